```python
import math
import jax
import jax.numpy as jnp
from jax import lax
import numpy as np

D_MODEL = 1024
BATCH = 4
SEQ = 8192
DEPTH = 2

CHUNK = 64
HEAD_DIM = 64
SSD_WIDTH = D_MODEL // 2
SSD_HEADS = SSD_WIDTH // HEAD_DIM
SSD_GROUPS = 2
SSD_STATE = 128
SSD_CONV = 4
SSD_CONV_DIM = SSD_WIDTH + 2 * SSD_GROUPS * SSD_STATE
GMLP_WIDTH = D_MODEL // 4
GMLP_GROUPS = GMLP_WIDTH // HEAD_DIM
GMLP_CHUNK = 128
ATT_WIDTH = D_MODEL // 4
ATT_HEADS = ATT_WIDTH // HEAD_DIM
ATT_PREV_CHUNKS = 8
ATT_BAND = (ATT_PREV_CHUNKS + 1) * CHUNK
ATT_MAX_REL = 128
D_MIX = SSD_WIDTH + GMLP_WIDTH + ATT_WIDTH
D_IN = SSD_WIDTH + SSD_CONV_DIM + SSD_HEADS + 2 * GMLP_WIDTH + 3 * ATT_WIDTH
_S1 = SSD_WIDTH
_S2 = _S1 + SSD_CONV_DIM
_S3 = _S2 + SSD_HEADS
_S4 = _S3 + GMLP_WIDTH
_S5 = _S4 + GMLP_WIDTH
IN_SPLITS = (_S1, _S2, _S3, _S4, _S5)
D_FF = 2816
N_EXPERTS = 8
TOP_K = 2
D_FF_EXPERT = 3584
MOE_BLOCK = 512
N_DENSE = (DEPTH + 1) // 2
N_MOE = DEPTH // 2
RMS_EPS = 1e-6
LN_EPS = 1e-5

kernel_name = "hybrid_ssd_gmlp_bandattn_moe_block"


def rms_norm(x, g):
    xf = x.astype(jnp.float32)
    y = xf * lax.rsqrt(jnp.mean(xf * xf, axis=-1, keepdims=True) + RMS_EPS)
    return (y * g.astype(jnp.float32)).astype(x.dtype)


def layer_norm(x, g, b):
    xf = x.astype(jnp.float32)
    mu = jnp.mean(xf, axis=-1, keepdims=True)
    xc = xf - mu
    var = jnp.mean(xc * xc, axis=-1, keepdims=True)
    return (xc * lax.rsqrt(var + LN_EPS) * g.astype(jnp.float32) + b.astype(jnp.float32)).astype(x.dtype)


def segsum(a):
    t = a.shape[-1]
    strict = jnp.tril(jnp.ones((t, t), bool), -1)
    x = jnp.where(strict, a[..., :, None], 0.0)
    x = jnp.cumsum(x, axis=-2)
    return jnp.where(jnp.tril(jnp.ones((t, t), bool)), x, -jnp.inf)


def ssd_mixer(z, xbc, dt_raw, conv_w, conv_b, dt_bias, a_log, d_skip, norm_g):
    f32 = jnp.float32
    b, s, _ = z.shape
    nc = s // CHUNK
    r = SSD_HEADS // SSD_GROUPS
    xbc = lax.conv_general_dilated(xbc, conv_w[:, None, :], (1,), [(SSD_CONV - 1, 0)],
                                   dimension_numbers=('NWC', 'WIO', 'NWC'),
                                   feature_group_count=SSD_CONV_DIM)
    xbc = jax.nn.silu(xbc + conv_b)
    xs, bm, cm = jnp.split(xbc.astype(f32), [SSD_WIDTH, SSD_WIDTH + SSD_GROUPS * SSD_STATE], axis=-1)
    x = xs.reshape(b, nc, CHUNK, SSD_GROUPS, r, HEAD_DIM)
    bm = bm.reshape(b, nc, CHUNK, SSD_GROUPS, SSD_STATE)
    cm = cm.reshape(b, nc, CHUNK, SSD_GROUPS, SSD_STATE)
    dt = jax.nn.softplus(dt_raw.astype(f32) + dt_bias.astype(f32)).reshape(b, nc, CHUNK, SSD_GROUPS, r)
    a = -jnp.exp(a_log.astype(f32)).reshape(SSD_GROUPS, r)
    da = (dt * a).transpose(0, 3, 4, 1, 2)
    xdt = x * dt[..., None]
    a_cs = jnp.cumsum(da, axis=-1)
    decay_in = jnp.exp(segsum(da))
    cb = jnp.einsum('bclgn,bcsgn->bgcls', cm, bm)
    y_diag = jnp.einsum('bgrcls,bcsgrp->bclgrp', cb[:, :, None] * decay_in, xdt)
    decay_to_end = jnp.exp(a_cs[..., -1:] - a_cs).transpose(0, 3, 4, 1, 2)
    states = jnp.einsum('bclgn,bclgrp->bcgrpn', bm, xdt * decay_to_end[..., None])
    states = jnp.concatenate([jnp.zeros_like(states[:, :1]), states], axis=1)
    chunk_decay = jnp.exp(segsum(jnp.pad(a_cs[..., -1], ((0, 0), (0, 0), (0, 0), (1, 0)))))
    states = jnp.einsum('bgrzc,bcgrpn->bzgrpn', chunk_decay, states)[:, :-1]
    decay_from_start = jnp.exp(a_cs).transpose(0, 3, 4, 1, 2)
    y_off = jnp.einsum('bclgn,bcgrpn->bclgrp', cm, states) * decay_from_start[..., None]
    y = y_diag + y_off + x * d_skip.astype(f32).reshape(SSD_GROUPS, r, 1)
    y = y.reshape(b, s, SSD_WIDTH) * jax.nn.silu(z.astype(f32))
    return rms_norm(y, norm_g).astype(z.dtype)


def spatial_gating(u, v, ln_g, ln_b, w_s, b_s):
    b, s, _ = u.shape
    n = s // GMLP_CHUNK
    v = layer_norm(v, ln_g, ln_b).reshape(b, n, GMLP_CHUNK, GMLP_GROUPS, HEAD_DIM)
    w = w_s * jnp.tril(jnp.ones((GMLP_CHUNK, GMLP_CHUNK), w_s.dtype))
    mixed = jnp.einsum('gts,bnsgc->bntgc', w, v) + b_s.T[:, :, None]
    return u * mixed.reshape(b, s, GMLP_WIDTH)


def band_attention(q, k, v, rel_bias):
    b, s, _ = q.shape
    nc = s // CHUNK
    q = q.reshape(b, nc, CHUNK, ATT_HEADS, HEAD_DIM)
    pad = ((0, 0), (ATT_PREV_CHUNKS * CHUNK, 0), (0, 0))
    kp = jnp.pad(k, pad).reshape(b, nc + ATT_PREV_CHUNKS, CHUNK, ATT_HEADS, HEAD_DIM)
    vp = jnp.pad(v, pad).reshape(b, nc + ATT_PREV_CHUNKS, CHUNK, ATT_HEADS, HEAD_DIM)
    kb = jnp.concatenate([kp[:, j:j + nc] for j in range(ATT_PREV_CHUNKS + 1)], axis=2)
    vb = jnp.concatenate([vp[:, j:j + nc] for j in range(ATT_PREV_CHUNKS + 1)], axis=2)
    scores = jnp.einsum('bcqhd,bckhd->bhcqk', q, kb).astype(jnp.float32) * (HEAD_DIM ** -0.5)
    qi = jnp.arange(CHUNK)[:, None]
    ki = jnp.arange(ATT_BAND)[None, :]
    rel = qi - ki + ATT_PREV_CHUNKS * CHUNK
    idx = jnp.clip(rel, -ATT_MAX_REL, ATT_MAX_REL) + ATT_MAX_REL
    bias = rel_bias[:, idx].astype(jnp.float32)
    key_pos = jnp.arange(nc)[:, None] * CHUNK + ki - ATT_PREV_CHUNKS * CHUNK
    valid = key_pos >= 0
    scores = scores + bias[None, :, None]
    scores = jnp.where(valid[None, None, :, None, :], scores, -1e30)
    p = jax.nn.softmax(scores, axis=-1).astype(v.dtype)
    out = jnp.einsum('bhcqk,bckhd->bcqhd', p, vb)
    return out.reshape(b, s, ATT_WIDTH)


def swiglu(x, w_gate, w_up, w_down):
    return (jax.nn.silu(x @ w_gate) * (x @ w_up)) @ w_down


def moe_swiglu(x, router_w, w_gate, w_up, w_down):
    b, s, d = x.shape
    xt = x.reshape(b * s, d)
    n_assign = b * s * TOP_K
    logits = (xt @ router_w).astype(jnp.float32)
    top_logits, top_idx = lax.top_k(logits, TOP_K)
    gates = jax.nn.softmax(top_logits, axis=-1)
    flat_e = top_idx.reshape(-1)
    flat_g = gates.reshape(-1)
    order = jnp.argsort(flat_e)
    sorted_e = flat_e[order]
    counts = jnp.bincount(flat_e, length=N_EXPERTS)
    padded = (counts + MOE_BLOCK - 1) // MOE_BLOCK * MOE_BLOCK
    pad_end = jnp.cumsum(padded)
    pad_start = pad_end - padded
    start = jnp.cumsum(counts) - counts
    dest = pad_start[sorted_e] + jnp.arange(n_assign) - start[sorted_e]
    n_blocks = -(-n_assign // MOE_BLOCK) + N_EXPERTS
    n_rows = n_blocks * MOE_BLOCK
    row_tok = jnp.zeros((n_rows,), jnp.int32).at[dest].set((order // TOP_K).astype(jnp.int32))
    row_gate = jnp.zeros((n_rows,), jnp.float32).at[dest].set(flat_g[order])
    block_first = jnp.arange(n_blocks) * MOE_BLOCK
    block_e = jnp.minimum(jnp.sum(pad_end[None, :] <= block_first[:, None], axis=1), N_EXPERTS - 1)

    def expert_block(args):
        tok, e = args
        xb = xt[tok]
        return swiglu(xb, w_gate[e], w_up[e], w_down[e])

    y = lax.map(expert_block, (row_tok.reshape(n_blocks, MOE_BLOCK), block_e))
    y = y.reshape(n_rows, d) * row_gate[:, None].astype(x.dtype)
    return jnp.zeros_like(xt).at[row_tok].add(y).reshape(b, s, d)


def setup_inputs(seed: int = 0) -> dict:
    key = jax.random.key(seed)
    ks = iter(jax.random.split(key, 40))
    f32 = jnp.float32

    def nrm(shape, scale):
        return scale * jax.random.normal(next(ks), shape, f32)

    def gain(shape):
        return 1.0 + nrm(shape, 0.05)

    dt0 = jnp.exp(jax.random.uniform(next(ks), (DEPTH, SSD_HEADS), f32, math.log(1e-3), math.log(1e-1)))
    return {
        'x': nrm((BATCH, SEQ, D_MODEL), 1.0),
        'mix_pre_norm': gain((DEPTH, D_MODEL)),
        'mix_post_norm': gain((DEPTH, D_MODEL)),
        'ffn_pre_norm': gain((DEPTH, D_MODEL)),
        'ffn_post_norm': gain((DEPTH, D_MODEL)),
        'w_in': nrm((DEPTH, D_MODEL, D_IN), D_MODEL ** -0.5),
        'ssd_conv_w': nrm((DEPTH, SSD_CONV, SSD_CONV_DIM), SSD_CONV ** -0.5),
        'ssd_conv_b': nrm((DEPTH, SSD_CONV_DIM), 0.02),
        'ssd_dt_bias': dt0 + jnp.log(-jnp.expm1(-dt0)),
        'ssd_a_log': jnp.log(jax.random.uniform(next(ks), (DEPTH, SSD_HEADS), f32, 1.0, 16.0)),
        'ssd_d': 1.0 + nrm((DEPTH, SSD_HEADS), 0.1),
        'ssd_norm': gain((DEPTH, SSD_WIDTH)),
        'gmlp_ln_g': gain((DEPTH, GMLP_WIDTH)),
        'gmlp_ln_b': nrm((DEPTH, GMLP_WIDTH), 0.02),
        'gmlp_w_s': nrm((DEPTH, GMLP_GROUPS, GMLP_CHUNK, GMLP_CHUNK), GMLP_CHUNK ** -0.5),
        'gmlp_b_s': 1.0 + nrm((DEPTH, GMLP_GROUPS, GMLP_CHUNK), 0.02),
        'gmlp_out_norm': gain((DEPTH, GMLP_WIDTH)),
        'attn_rel_bias': nrm((DEPTH, ATT_HEADS, 2 * ATT_MAX_REL + 1), 0.2),
        'attn_out_norm': gain((DEPTH, ATT_WIDTH)),
        'w_out': nrm((DEPTH, D_MIX, D_MODEL), D_MIX ** -0.5),
        'ffn_w_gate': nrm((N_DENSE, D_MODEL, D_FF), D_MODEL ** -0.5),
        'ffn_w_up': nrm((N_DENSE, D_MODEL, D_FF), D_MODEL ** -0.5),
        'ffn_w_down': nrm((N_DENSE, D_FF, D_MODEL), D_FF ** -0.5),
        'moe_router': nrm((N_MOE, D_MODEL, N_EXPERTS), D_MODEL ** -0.5),
        'moe_w_gate': nrm((N_MOE, N_EXPERTS, D_MODEL, D_FF_EXPERT), D_MODEL ** -0.5),
        'moe_w_up': nrm((N_MOE, N_EXPERTS, D_MODEL, D_FF_EXPERT), D_MODEL ** -0.5),
        'moe_w_down': nrm((N_MOE, N_EXPERTS, D_FF_EXPERT, D_MODEL), D_FF_EXPERT ** -0.5),
    }


def reference(x, mix_pre_norm, mix_post_norm, ffn_pre_norm, ffn_post_norm, w_in,
              ssd_conv_w, ssd_conv_b, ssd_dt_bias, ssd_a_log, ssd_d, ssd_norm,
              gmlp_ln_g, gmlp_ln_b, gmlp_w_s, gmlp_b_s, gmlp_out_norm,
              attn_rel_bias, attn_out_norm, w_out,
              ffn_w_gate, ffn_w_up, ffn_w_down,
              moe_router, moe_w_gate, moe_w_up, moe_w_down):
    h = x
    for layer in range(DEPTH):
        a = rms_norm(h, mix_pre_norm[layer])
        proj = a @ w_in[layer]
        z, xbc, dt_raw, u, v, qkv = jnp.split(proj, IN_SPLITS, axis=-1)
        q, k, vv = jnp.split(qkv, 3, axis=-1)
        y_ssd = ssd_mixer(z, xbc, dt_raw, ssd_conv_w[layer], ssd_conv_b[layer], ssd_dt_bias[layer],
                          ssd_a_log[layer], ssd_d[layer], ssd_norm[layer])
        y_gmlp = rms_norm(spatial_gating(u, v, gmlp_ln_g[layer], gmlp_ln_b[layer],
                                         gmlp_w_s[layer], gmlp_b_s[layer]), gmlp_out_norm[layer])
        y_att = rms_norm(band_attention(q, k, vv, attn_rel_bias[layer]), attn_out_norm[layer])
        mixed = jnp.concatenate([y_ssd, y_gmlp, y_att], axis=-1) @ w_out[layer]
        h = h + rms_norm(mixed, mix_post_norm[layer])
        f = rms_norm(h, ffn_pre_norm[layer])
        if layer % 2 == 0:
            i = layer // 2
            f = swiglu(f, ffn_w_gate[i], ffn_w_up[i], ffn_w_down[i])
        else:
            i = layer // 2
            f = moe_swiglu(f, moe_router[i], moe_w_gate[i], moe_w_up[i], moe_w_down[i])
        h = h + rms_norm(f, ffn_post_norm[layer])
    return h
```

```python
import functools

import jax
import jax.numpy as jnp
from jax import lax
from jax.experimental import pallas as pl
from jax.experimental.pallas import tpu as pltpu

F32 = jnp.float32
BF16 = jnp.bfloat16
HIGHEST = lax.Precision.HIGHEST

HEAD_DIM = 64
CHUNK = 64
SSD_WIDTH = 512
SSD_HEADS = 8
SSD_STATE = 128
SSD_CONV = 4
SSD_CONV_DIM = 1024
GMLP_WIDTH = 256
GMLP_GROUPS = 4
GMLP_CHUNK = 128
ATT_WIDTH = 256
ATT_HEADS = 4
ATT_PREV_CHUNKS = 8
ATT_MAX_REL = 128
N_EXPERTS = 8
TOP_K = 2
RMS_EPS = 1e-6
LN_EPS = 1e-5
NEG = -1e30

LANES = 128
VMEM_LIMIT = 56 * 1024 * 1024

_C_Z = 0
_C_XBC = _C_Z + SSD_WIDTH
_C_DT = _C_XBC + SSD_CONV_DIM
_C_U = _C_DT + LANES
_C_V = _C_U + GMLP_WIDTH
_C_Q = _C_V + GMLP_WIDTH
_C_K = _C_Q + ATT_WIDTH
_C_VV = _C_K + ATT_WIDTH
_C_END = _C_VV + ATT_WIDTH

ROW_TILE = 512
SSD_TILE = 512
SSD_L = 256
GMLP_TILE = 512
ATT_TILE = 256
MOE_TM = 512
MOE_TF = 512
FFN_FC = 256


def _params(sem):
    return pltpu.CompilerParams(dimension_semantics=sem, vmem_limit_bytes=VMEM_LIMIT)


def _rms(x, g):
    y = x * lax.rsqrt(jnp.mean(x * x, axis=-1, keepdims=True) + RMS_EPS)
    return y * g


def _silu(x):
    return x * (1.0 / (1.0 + jnp.exp(-x)))


def _dot(a, b):
    return jnp.dot(a, b, preferred_element_type=F32)


def _dot_nt(a, b):
    return lax.dot_general(a, b, (((1,), (1,)), ((), ())), preferred_element_type=F32)


def _dot_tn(a, b, precision=None):
    return lax.dot_general(a, b, (((0,), (0,)), ((), ())), preferred_element_type=F32,
                           precision=precision)


def _inproj_body(h_ref, g_ref, w_ref, z_ref, xbc_ref, dt_ref, u_ref, v_ref, q_ref, k_ref, vv_ref):
    a = _rms(h_ref[...], g_ref[...]).astype(BF16)

    def proj(lo, hi):
        return _dot(a, w_ref[:, lo:hi])

    z_ref[...] = proj(_C_Z, _C_XBC).astype(BF16)
    xbc_ref[...] = proj(_C_XBC, _C_DT).astype(BF16)
    dt_ref[...] = proj(_C_DT, _C_U)
    u_ref[...] = proj(_C_U, _C_V).astype(BF16)
    v_ref[...] = proj(_C_V, _C_Q).astype(BF16)
    q_ref[...] = proj(_C_Q, _C_K).astype(BF16)
    k_ref[...] = proj(_C_K, _C_VV).astype(BF16)
    vv_ref[...] = proj(_C_VV, _C_END).astype(BF16)


def _in_proj(h, g, w_all, tm=ROW_TILE):
    n, d = h.shape
    widths = [(SSD_WIDTH, BF16), (SSD_CONV_DIM, BF16), (LANES, F32), (GMLP_WIDTH, BF16),
              (GMLP_WIDTH, BF16), (ATT_WIDTH, BF16), (ATT_WIDTH, BF16), (ATT_WIDTH, BF16)]
    row = lambda i: (i, 0)
    full = lambda i: (0, 0)
    return pl.pallas_call(
        _inproj_body,
        grid=(n // tm,),
        in_specs=[pl.BlockSpec((tm, d), row), pl.BlockSpec((1, d), full),
                  pl.BlockSpec(w_all.shape, full)],
        out_specs=[pl.BlockSpec((tm, w), row) for w, _ in widths],
        out_shape=[jax.ShapeDtypeStruct((n, w), dt) for w, dt in widths],
        compiler_params=_params(("arbitrary",)),
        name="in_proj",
    )(h, g, w_all)


def _expand4(q, g, lane):
    c = [q[:, 4 * g + i:4 * g + i + 1] for i in range(4)]
    return jnp.where(lane < 64, c[0], jnp.where(lane < 128, c[1], jnp.where(lane < 192, c[2], c[3])))


def _ssd_body(z_ref, xbc_ref, dt_ref, cw_ref, cbias_ref, dtb_ref, alog_ref, dsk_ref, ng_ref,
              y_ref, h_scr, tail_scr, xe_scr, xc_scr, yacc_scr, *, chunk):
    t = xbc_ref.shape[0]
    gw = SSD_WIDTH // 2

    @pl.when(pl.program_id(1) == 0)
    def _():
        h_scr[...] = jnp.zeros_like(h_scr)
        tail_scr[...] = jnp.zeros_like(tail_scr)

    xcur = xbc_ref[...].astype(F32)
    xe_scr[0:8, :] = tail_scr[...]
    xe_scr[8:8 + t, :] = xcur
    tail_scr[...] = xcur[t - 8:t, :]
    acc = cbias_ref[...] + cw_ref[SSD_CONV - 1:SSD_CONV, :] * xcur
    for k in range(SSD_CONV - 1):
        acc = acc + cw_ref[k:k + 1, :] * xe_scr[pl.ds(8 - (SSD_CONV - 1) + k, t), :]
    xc_scr[...] = _silu(acc)

    dt = dt_ref[...] + dtb_ref[...]
    dt = jnp.maximum(dt, 0.0) + jnp.log1p(jnp.exp(-jnp.abs(dt)))
    da = dt * (-jnp.exp(alog_ref[...]))

    li = lax.broadcasted_iota(jnp.int32, (chunk, chunk), 0)
    si = lax.broadcasted_iota(jnp.int32, (chunk, chunk), 1)
    tril = li >= si
    tri_f = tril.astype(F32)
    lane = lax.broadcasted_iota(jnp.int32, (chunk, gw), 1)
    head_of_lane = lane // HEAD_DIM
    prow = lax.broadcasted_iota(jnp.int32, (gw, LANES), 0) // HEAD_DIM
    pcol = lax.broadcasted_iota(jnp.int32, (gw, LANES), 1)

    for c in range(t // chunk):
        r0 = c * chunk
        da_c = da[r0:r0 + chunk, :]
        dt_c = dt[r0:r0 + chunk, :]
        acs = jnp.dot(tri_f, da_c, preferred_element_type=F32, precision=HIGHEST)
        acs_t = acs.T
        a_tot = acs[chunk - 1:chunk, :]
        e_acs = jnp.exp(acs)
        w_end = dt_c * jnp.exp(a_tot - acs)
        e_tot = jnp.exp(a_tot)
        for g in range(2):
            b0 = SSD_WIDTH + g * SSD_STATE
            c0 = SSD_WIDTH + (2 + g) * SSD_STATE
            xg = xc_scr[r0:r0 + chunk, g * gw:(g + 1) * gw]
            bg = xc_scr[r0:r0 + chunk, b0:b0 + SSD_STATE].astype(BF16)
            cg = xc_scr[r0:r0 + chunk, c0:c0 + SSD_STATE].astype(BF16)
            cb = _dot_nt(cg, bg)
            xdt = xg * _expand4(dt_c, g, lane)
            y = xg * dsk_ref[:, g * gw:(g + 1) * gw]
            for hh in range(4):
                h = 4 * g + hh
                diff = acs[:, h:h + 1] - acs_t[h:h + 1, :]
                m_h = (cb * jnp.where(tril, jnp.exp(diff), 0.0)).astype(BF16)
                x_h = jnp.where(head_of_lane == hh, xdt, 0.0).astype(BF16)
                y = y + _dot(m_h, x_h)
            h_old = h_scr[g]
            y = y + _dot_nt(cg, h_old.astype(BF16)) * _expand4(e_acs, g, lane)
            xw = (xg * _expand4(w_end, g, lane)).astype(BF16)
            s_new = _dot_tn(xw, bg)
            rowscale = jnp.sum(jnp.where(pcol == prow + 4 * g, e_tot, 0.0), axis=1, keepdims=True)
            h_scr[g] = h_old * rowscale + s_new
            zg = z_ref[r0:r0 + chunk, g * gw:(g + 1) * gw].astype(F32)
            yacc_scr[r0:r0 + chunk, g * gw:(g + 1) * gw] = y * _silu(zg)

    y_ref[...] = _rms(yacc_scr[...], ng_ref[...]).astype(y_ref.dtype)


def _ssd(z, xbc, dt, conv_w, conv_b, dt_bias, a_log, d_lane, norm_g, batch, seq,
         tile=SSD_TILE, chunk=SSD_L):
    n = batch * seq
    nt = seq // tile
    row = lambda b, i: (b * nt + i, 0)
    full = lambda b, i: (0, 0)
    return pl.pallas_call(
        functools.partial(_ssd_body, chunk=chunk),
        grid=(batch, nt),
        in_specs=[pl.BlockSpec((tile, SSD_WIDTH), row), pl.BlockSpec((tile, SSD_CONV_DIM), row),
                  pl.BlockSpec((tile, LANES), row),
                  pl.BlockSpec((SSD_CONV, SSD_CONV_DIM), full), pl.BlockSpec((1, SSD_CONV_DIM), full),
                  pl.BlockSpec((1, LANES), full), pl.BlockSpec((1, LANES), full),
                  pl.BlockSpec((1, SSD_WIDTH), full), pl.BlockSpec((1, SSD_WIDTH), full)],
        out_specs=pl.BlockSpec((tile, SSD_WIDTH), row),
        out_shape=jax.ShapeDtypeStruct((n, SSD_WIDTH), BF16),
        scratch_shapes=[pltpu.VMEM((2, SSD_WIDTH // 2, SSD_STATE), F32),
                        pltpu.VMEM((8, SSD_CONV_DIM), F32),
                        pltpu.VMEM((tile + 8, SSD_CONV_DIM), F32),
                        pltpu.VMEM((tile, SSD_CONV_DIM), F32),
                        pltpu.VMEM((tile, SSD_WIDTH), F32)],
        compiler_params=_params(("arbitrary", "arbitrary")),
        name="ssd",
    )(z, xbc, dt, conv_w, conv_b, dt_bias, a_log, d_lane, norm_g)


def _gmlp_body(u_ref, v_ref, lng_ref, lnb_ref, ws_ref, bias_ref, og_ref, y_ref):
    t = u_ref.shape[0]
    blk = GMLP_CHUNK
    ti = lax.broadcasted_iota(jnp.int32, (blk, blk), 0)
    si = lax.broadcasted_iota(jnp.int32, (blk, blk), 1)
    w = [jnp.where(ti >= si, ws_ref[g], 0.0).astype(BF16) for g in range(GMLP_GROUPS)]
    grp = lax.broadcasted_iota(jnp.int32, (blk, GMLP_WIDTH), 1) // HEAD_DIM
    for c in range(t // blk):
        r0 = c * blk
        v = v_ref[r0:r0 + blk, :].astype(F32)
        mu = jnp.mean(v, axis=-1, keepdims=True)
        vc = v - mu
        var = jnp.mean(vc * vc, axis=-1, keepdims=True)
        vn = (vc * lax.rsqrt(var + LN_EPS) * lng_ref[...] + lnb_ref[...]).astype(BF16)
        mixed = bias_ref[...]
        for g in range(GMLP_GROUPS):
            mixed = mixed + jnp.where(grp == g, _dot(w[g], vn), 0.0)
        out = u_ref[r0:r0 + blk, :].astype(F32) * mixed
        y_ref[r0:r0 + blk, :] = _rms(out, og_ref[...]).astype(y_ref.dtype)


def _gmlp(u, v, ln_g, ln_b, w_s, bias_tc, out_g, tile=GMLP_TILE):
    n = u.shape[0]
    row = lambda i: (i, 0)
    full2 = lambda i: (0, 0)
    return pl.pallas_call(
        _gmlp_body,
        grid=(n // tile,),
        in_specs=[pl.BlockSpec((tile, GMLP_WIDTH), row), pl.BlockSpec((tile, GMLP_WIDTH), row),
                  pl.BlockSpec((1, GMLP_WIDTH), full2), pl.BlockSpec((1, GMLP_WIDTH), full2),
                  pl.BlockSpec(w_s.shape, lambda i: (0, 0, 0)),
                  pl.BlockSpec((GMLP_CHUNK, GMLP_WIDTH), full2), pl.BlockSpec((1, GMLP_WIDTH), full2)],
        out_specs=pl.BlockSpec((tile, GMLP_WIDTH), row),
        out_shape=jax.ShapeDtypeStruct((n, GMLP_WIDTH), BF16),
        compiler_params=_params(("arbitrary",)),
        name="gmlp",
    )(u, v, ln_g, ln_b, w_s, bias_tc, out_g)


def _attn_body(q_ref, k0_ref, k1_ref, k2_ref, v0_ref, v1_ref, v2_ref, bias_ref, og_ref, y_ref):
    tq = q_ref.shape[0]
    nk = 3 * tq
    start = pl.program_id(1) * tq
    q = q_ref[...]
    kcat = jnp.concatenate([k0_ref[...], k1_ref[...], k2_ref[...]], axis=0)
    vcat = jnp.concatenate([v0_ref[...], v1_ref[...], v2_ref[...]], axis=0)
    kk = lax.broadcasted_iota(jnp.int32, (tq, nk), 1)
    key_ok = kk >= 2 * tq - start
    head_of_lane = lax.broadcasted_iota(jnp.int32, (tq, ATT_WIDTH), 1) // HEAD_DIM
    out = jnp.zeros((tq, ATT_WIDTH), F32)
    for h in range(ATT_HEADS):
        qh = jnp.where(head_of_lane == h, q, jnp.zeros_like(q))
        s = _dot_nt(qh, kcat) + bias_ref[h]
        s = jnp.where(key_ok, s, NEG)
        m = jnp.max(s, axis=-1, keepdims=True)
        p = jnp.exp(s - m)
        l = jnp.sum(p, axis=-1, keepdims=True)
        o = _dot(p.astype(BF16), vcat) / l
        out = jnp.where(head_of_lane == h, o, out)
    y_ref[...] = _rms(out, og_ref[...]).astype(y_ref.dtype)


def _attn(q, k, v, bias_tab, out_g, batch, seq, tile=ATT_TILE):
    n = batch * seq
    nt = seq // tile
    cur = lambda b, i: (b * nt + i, 0)
    prev1 = lambda b, i: (b * nt + jnp.maximum(i - 1, 0), 0)
    prev2 = lambda b, i: (b * nt + jnp.maximum(i - 2, 0), 0)
    blk = lambda m: pl.BlockSpec((tile, ATT_WIDTH), m)
    return pl.pallas_call(
        _attn_body,
        grid=(batch, nt),
        in_specs=[blk(cur), blk(prev2), blk(prev1), blk(cur), blk(prev2), blk(prev1), blk(cur),
                  pl.BlockSpec(bias_tab.shape, lambda b, i: (0, 0, 0)),
                  pl.BlockSpec((1, ATT_WIDTH), lambda b, i: (0, 0))],
        out_specs=blk(cur),
        out_shape=jax.ShapeDtypeStruct((n, ATT_WIDTH), BF16),
        compiler_params=_params(("arbitrary", "arbitrary")),
        name="band_attn",
    )(q, k, k, k, v, v, v, bias_tab, out_g)


def _attn_bias_table(rel_bias, tile=ATT_TILE):
    qq = jnp.arange(tile)[:, None]
    kk = jnp.arange(3 * tile)[None, :]
    rel = qq - kk + 2 * tile
    idx = jnp.clip(rel, -ATT_MAX_REL, ATT_MAX_REL) + ATT_MAX_REL
    qc = qq // CHUNK + (2 * tile) // CHUNK
    kc = kk // CHUNK
    band = (kc <= qc) & (kc >= qc - ATT_PREV_CHUNKS)
    return jnp.where(band[None], rel_bias[:, idx].astype(F32), NEG)


def _outproj_body(*refs, with_router):
    if with_router:
        (ys_ref, yg_ref, ya_ref, h_ref, w_ref, gpost_ref, gpre_ref, rw_ref,
         hn_ref, f_ref, idx_ref, gate_ref) = refs
    else:
        ys_ref, yg_ref, ya_ref, h_ref, w_ref, gpost_ref, gpre_ref, hn_ref, f_ref = refs
    s1 = SSD_WIDTH
    s2 = s1 + GMLP_WIDTH
    mixed = (_dot(ys_ref[...], w_ref[0:s1, :]) + _dot(yg_ref[...], w_ref[s1:s2, :])
             + _dot(ya_ref[...], w_ref[s2:, :]))
    hn = h_ref[...] + _rms(mixed, gpost_ref[...])
    hn_ref[...] = hn
    f = _rms(hn, gpre_ref[...])
    f_ref[...] = f.astype(f_ref.dtype)
    if with_router:
        logits = jnp.dot(f, rw_ref[...], preferred_element_type=F32, precision=HIGHEST)
        lane = lax.broadcasted_iota(jnp.int32, logits.shape, 1)
        lane_f = lane.astype(F32)
        logits = jnp.where(lane < N_EXPERTS, logits, -jnp.inf)
        t1 = jnp.max(logits, axis=-1, keepdims=True)
        i1 = jnp.min(jnp.where(logits == t1, lane_f, float(LANES)), axis=-1, keepdims=True)
        rest = jnp.where(lane_f == i1, -jnp.inf, logits)
        t2 = jnp.max(rest, axis=-1, keepdims=True)
        i2 = jnp.min(jnp.where(rest == t2, lane_f, float(LANES)), axis=-1, keepdims=True)
        e2 = jnp.exp(t2 - t1)
        g1 = 1.0 / (1.0 + e2)
        g2 = e2 / (1.0 + e2)
        idx_ref[...] = jnp.where(lane == 0, i1, i2).astype(jnp.int32)
        gate_ref[...] = jnp.where(lane == 0, g1, g2)


def _out_proj(ys, yg, ya, h, w_out, g_post, g_pre, router_w=None, tm=ROW_TILE):
    n, d = h.shape
    with_router = router_w is not None
    row = lambda i: (i, 0)
    full = lambda i: (0, 0)
    in_specs = [pl.BlockSpec((tm, SSD_WIDTH), row), pl.BlockSpec((tm, GMLP_WIDTH), row),
                pl.BlockSpec((tm, ATT_WIDTH), row), pl.BlockSpec((tm, d), row),
                pl.BlockSpec(w_out.shape, full), pl.BlockSpec((1, d), full), pl.BlockSpec((1, d), full)]
    out_specs = [pl.BlockSpec((tm, d), row), pl.BlockSpec((tm, d), row)]
    out_shape = [jax.ShapeDtypeStruct((n, d), F32),
                 jax.ShapeDtypeStruct((n, d), F32 if with_router else BF16)]
    args = [ys, yg, ya, h, w_out, g_post, g_pre]
    if with_router:
        in_specs.append(pl.BlockSpec(router_w.shape, full))
        out_specs += [pl.BlockSpec((tm, LANES), row), pl.BlockSpec((tm, LANES), row)]
        out_shape += [jax.ShapeDtypeStruct((n, LANES), jnp.int32), jax.ShapeDtypeStruct((n, LANES), F32)]
        args.append(router_w)
    return pl.pallas_call(
        functools.partial(_outproj_body, with_router=with_router),
        grid=(n // tm,),
        in_specs=in_specs, out_specs=out_specs, out_shape=out_shape,
        compiler_params=_params(("arbitrary",)),
        name="out_proj_router" if with_router else "out_proj",
    )(*args)


def _ffn_body(f_ref, h_ref, wg_ref, wu_ref, wd_ref, gpost_ref, o_ref, acc_ref, *, fc):
    x = f_ref[...]
    d_ff = wg_ref.shape[1]
    for c in range(d_ff // fc):
        lo = c * fc
        mid = (_silu(_dot(x, wg_ref[:, lo:lo + fc])) * _dot(x, wu_ref[:, lo:lo + fc])).astype(BF16)
        part = _dot(mid, wd_ref[lo:lo + fc, :])
        if c == 0:
            acc_ref[...] = part
        else:
            acc_ref[...] += part
    o_ref[...] = h_ref[...] + _rms(acc_ref[...], gpost_ref[...])


def _ffn_dense(f, h, wg, wu, wd, g_post, tm=ROW_TILE, fc=FFN_FC):
    n, d = h.shape
    row = lambda i: (i, 0)
    full = lambda i: (0, 0)
    return pl.pallas_call(
        functools.partial(_ffn_body, fc=fc),
        grid=(n // tm,),
        in_specs=[pl.BlockSpec((tm, d), row), pl.BlockSpec((tm, d), row),
                  pl.BlockSpec(wg.shape, full), pl.BlockSpec(wu.shape, full),
                  pl.BlockSpec(wd.shape, full), pl.BlockSpec((1, d), full)],
        out_specs=pl.BlockSpec((tm, d), row),
        out_shape=jax.ShapeDtypeStruct((n, d), F32),
        scratch_shapes=[pltpu.VMEM((tm, d), F32)],
        compiler_params=_params(("arbitrary",)),
        name="ffn_dense",
    )(f, h, wg, wu, wd, g_post)


def _row_copy(src_hbm, idx, dst, r, sem):
    return pltpu.make_async_copy(src_hbm.at[pl.ds(idx, 1)], dst.at[pl.ds(r, 1)], sem)


def _moe_body(te_ref, nu_ref, tok_ref, tokn_ref, x_hbm, wg_ref, wu_ref, wd_ref, y_ref,
              xbuf, xb16, sem, acc_ref):
    i = pl.program_id(0)
    j = pl.program_id(1)
    tm = xb16.shape[0]
    n_used = nu_ref[0]
    slot = lax.rem(i, 2)
    used = i < n_used

    def issue(idx_ref, s):
        def body(r, carry):
            _row_copy(x_hbm, idx_ref[0, 0, r], xbuf.at[s], r, sem.at[s]).start()
            return carry
        lax.fori_loop(0, tm, body, 0)

    def wait(s):
        def body(r, carry):
            _row_copy(x_hbm, 0, xbuf.at[s], r, sem.at[s]).wait()
            return carry
        lax.fori_loop(0, tm, body, 0)

    @pl.when(jnp.logical_and(used, j == 0))
    def _():
        @pl.when(i == 0)
        def _():
            issue(tok_ref, 0)
        wait(slot)

        @pl.when(i + 1 < n_used)
        def _():
            issue(tokn_ref, 1 - slot)
        xb16[...] = xbuf[slot].astype(BF16)

    @pl.when(used)
    def _():
        x = xb16[...]
        mid = (_silu(_dot(x, wg_ref[0])) * _dot(x, wu_ref[0])).astype(BF16)
        part = _dot(mid, wd_ref[0])

        @pl.when(j == 0)
        def _():
            acc_ref[...] = part

        @pl.when(j > 0)
        def _():
            acc_ref[...] += part

    @pl.when(jnp.logical_and(used, j == pl.num_programs(1) - 1))
    def _():
        y_ref[...] = acc_ref[...]

    @pl.when(jnp.logical_and(jnp.logical_not(used), j == 0))
    def _():
        y_ref[...] = jnp.zeros_like(y_ref)


def _moe_experts(tile_expert, n_used, row_tok, x, wg, wu, wd, tm=MOE_TM, tf=MOE_TF):
    n_tiles = row_tok.shape[0]
    d = x.shape[1]
    d_ff = wg.shape[2]
    last = n_tiles - 1
    tok_spec = lambda m: pl.BlockSpec((1, 1, tm), m, memory_space=pltpu.SMEM)
    grid_spec = pltpu.PrefetchScalarGridSpec(
        num_scalar_prefetch=2,
        grid=(n_tiles, d_ff // tf),
        in_specs=[tok_spec(lambda i, j, te, nu: (i, 0, 0)),
                  tok_spec(lambda i, j, te, nu: (jnp.minimum(i + 1, last), 0, 0)),
                  pl.BlockSpec(memory_space=pl.ANY),
                  pl.BlockSpec((1, d, tf), lambda i, j, te, nu: (te[i], 0, j)),
                  pl.BlockSpec((1, d, tf), lambda i, j, te, nu: (te[i], 0, j)),
                  pl.BlockSpec((1, tf, d), lambda i, j, te, nu: (te[i], j, 0))],
        out_specs=pl.BlockSpec((tm, d), lambda i, j, te, nu: (i, 0)),
        scratch_shapes=[pltpu.VMEM((2, tm, d), F32), pltpu.VMEM((tm, d), BF16),
                        pltpu.SemaphoreType.DMA((2,)), pltpu.VMEM((tm, d), F32)],
    )
    return pl.pallas_call(
        _moe_body,
        grid_spec=grid_spec,
        out_shape=jax.ShapeDtypeStruct((n_tiles * tm, d), F32),
        compiler_params=_params(("arbitrary", "arbitrary")),
        name="moe_experts",
    )(tile_expert, n_used, row_tok, row_tok, x, wg, wu, wd)


def _combine_body(d0_ref, d0n_ref, d1_ref, d1n_ref, y_hbm, gate_ref, h_ref, gpost_ref, o_ref,
                  buf, sem):
    i = pl.program_id(0)
    tm = o_ref.shape[0]
    slot = lax.rem(i, 2)

    def issue(a_ref, b_ref, s):
        def body(r, carry):
            _row_copy(y_hbm, a_ref[0, 0, r], buf.at[s, 0], r, sem.at[s]).start()
            _row_copy(y_hbm, b_ref[0, 0, r], buf.at[s, 1], r, sem.at[s]).start()
            return carry
        lax.fori_loop(0, tm, body, 0)

    def wait(s):
        def body(r, carry):
            _row_copy(y_hbm, 0, buf.at[s, 0], r, sem.at[s]).wait()
            _row_copy(y_hbm, 0, buf.at[s, 1], r, sem.at[s]).wait()
            return carry
        lax.fori_loop(0, tm, body, 0)

    @pl.when(i == 0)
    def _():
        issue(d0_ref, d1_ref, 0)
    wait(slot)

    @pl.when(i + 1 < pl.num_programs(0))
    def _():
        issue(d0n_ref, d1n_ref, 1 - slot)

    gate = gate_ref[...]
    f = buf[slot, 0] * gate[:, 0:1] + buf[slot, 1] * gate[:, 1:2]
    o_ref[...] = h_ref[...] + _rms(f, gpost_ref[...])


def _moe_combine(dest0, dest1, y_rows, gates, h, g_post, tm=ROW_TILE):
    n, d = h.shape
    nt = n // tm
    last = nt - 1
    smem = lambda m: pl.BlockSpec((1, 1, tm), m, memory_space=pltpu.SMEM)
    cur3 = lambda i: (i, 0, 0)
    nxt3 = lambda i: (jnp.minimum(i + 1, last), 0, 0)
    row = lambda i: (i, 0)
    return pl.pallas_call(
        _combine_body,
        grid=(nt,),
        in_specs=[smem(cur3), smem(nxt3), smem(cur3), smem(nxt3),
                  pl.BlockSpec(memory_space=pl.ANY),
                  pl.BlockSpec((tm, LANES), row), pl.BlockSpec((tm, d), row),
                  pl.BlockSpec((1, d), lambda i: (0, 0))],
        out_specs=pl.BlockSpec((tm, d), row),
        out_shape=jax.ShapeDtypeStruct((n, d), F32),
        scratch_shapes=[pltpu.VMEM((2, 2, tm, d), F32), pltpu.SemaphoreType.DMA((2,))],
        compiler_params=_params(("arbitrary",)),
        name="moe_combine",
    )(dest0.reshape(nt, 1, tm), dest0.reshape(nt, 1, tm), dest1.reshape(nt, 1, tm),
      dest1.reshape(nt, 1, tm), y_rows, gates, h, g_post)


def _moe_layer(f, h, top_idx, gates, wg, wu, wd, g_post, tm=MOE_TM):
    n = f.shape[0]
    n_assign = n * TOP_K
    flat_e = top_idx.reshape(-1)
    onehot = (flat_e[:, None] == jnp.arange(N_EXPERTS, dtype=jnp.int32)[None, :]).astype(jnp.int32)
    csum = jnp.cumsum(onehot, axis=0)
    counts = csum[-1]
    rank = jnp.sum((csum - onehot) * onehot, axis=1)
    padded = (counts + tm - 1) // tm * tm
    pad_end = jnp.cumsum(padded)
    pad_start = pad_end - padded
    dest = (pad_start[flat_e] + rank).astype(jnp.int32)
    n_tiles = n_assign // tm + N_EXPERTS
    tok = (jnp.arange(n_assign, dtype=jnp.int32) // TOP_K)
    row_tok = jnp.zeros((n_tiles * tm,), jnp.int32).at[dest].set(tok).reshape(n_tiles, 1, tm)
    tile_first = jnp.arange(n_tiles, dtype=jnp.int32) * tm
    tile_expert = jnp.minimum(jnp.sum(pad_end[None, :] <= tile_first[:, None], axis=1),
                              N_EXPERTS - 1).astype(jnp.int32)
    n_used = (pad_end[-1] // tm).astype(jnp.int32)
    tile_expert = jnp.where(jnp.arange(n_tiles) < n_used, tile_expert, tile_expert[n_used - 1])
    n_used = n_used.reshape(1)
    y_rows = _moe_experts(tile_expert, n_used, row_tok, f, wg, wu, wd, tm=tm)
    dest2 = dest.reshape(n, TOP_K)
    return _moe_combine(dest2[:, 0], dest2[:, 1], y_rows, gates, h, g_post)


def _pad_lanes(x, width=LANES):
    return jnp.pad(x, ((0, 0), (0, width - x.shape[-1])))


def kernel(x, mix_pre_norm, mix_post_norm, ffn_pre_norm, ffn_post_norm, w_in, ssd_conv_w, ssd_conv_b, ssd_dt_bias, ssd_a_log, ssd_d, ssd_norm, gmlp_ln_g, gmlp_ln_b, gmlp_w_s, gmlp_b_s, gmlp_out_norm, attn_rel_bias, attn_out_norm, w_out, ffn_w_gate, ffn_w_up, ffn_w_down, moe_router, moe_w_gate, moe_w_up, moe_w_down):
    batch, seq, d = x.shape
    depth = w_in.shape[0]
    n = batch * seq
    h = x.reshape(n, d)
    s1 = SSD_WIDTH
    s2 = s1 + SSD_CONV_DIM
    s3 = s2 + SSD_HEADS
    s4 = s3 + GMLP_WIDTH
    s5 = s4 + GMLP_WIDTH
    for layer in range(depth):
        wl = w_in[layer]
        w_all = jnp.concatenate([
            wl[:, :s2], _pad_lanes(wl[:, s2:s3]), wl[:, s3:s5],
            wl[:, s5:s5 + ATT_WIDTH] * (HEAD_DIM ** -0.5), wl[:, s5 + ATT_WIDTH:]], axis=1).astype(BF16)
        z, xbc, dt, u, v, q, k, vv = _in_proj(h, mix_pre_norm[layer][None], w_all)
        y_ssd = _ssd(z, xbc, dt, ssd_conv_w[layer], ssd_conv_b[layer][None],
                     _pad_lanes(ssd_dt_bias[layer][None]), _pad_lanes(ssd_a_log[layer][None]),
                     jnp.repeat(ssd_d[layer], HEAD_DIM)[None], ssd_norm[layer][None], batch, seq)
        y_gmlp = _gmlp(u, v, gmlp_ln_g[layer][None], gmlp_ln_b[layer][None], gmlp_w_s[layer],
                       jnp.repeat(gmlp_b_s[layer].T, HEAD_DIM, axis=1), gmlp_out_norm[layer][None])
        y_att = _attn(q, k, vv, _attn_bias_table(attn_rel_bias[layer]), attn_out_norm[layer][None],
                      batch, seq)
        wo = w_out[layer].astype(BF16)
        i = layer // 2
        if layer % 2 == 0:
            h, f = _out_proj(y_ssd, y_gmlp, y_att, h, wo, mix_post_norm[layer][None],
                             ffn_pre_norm[layer][None])
            h = _ffn_dense(f, h, ffn_w_gate[i].astype(BF16), ffn_w_up[i].astype(BF16),
                           ffn_w_down[i].astype(BF16), ffn_post_norm[layer][None])
        else:
            h, f, idx, gate = _out_proj(y_ssd, y_gmlp, y_att, h, wo, mix_post_norm[layer][None],
                                        ffn_pre_norm[layer][None], _pad_lanes(moe_router[i]))
            h = _moe_layer(f, h, idx[:, :TOP_K], gate, moe_w_gate[i].astype(BF16),
                           moe_w_up[i].astype(BF16), moe_w_down[i].astype(BF16),
                           ffn_post_norm[layer][None])
    return h.reshape(batch, seq, d)
```

```python
import functools

import jax
import jax.numpy as jnp
from jax import lax
from jax.experimental import pallas as pl
from jax.experimental.pallas import tpu as pltpu

F32 = jnp.float32
BF16 = jnp.bfloat16
HIGHEST = lax.Precision.HIGHEST

HEAD_DIM = 64
CHUNK = 64
SSD_WIDTH = 512
SSD_HEADS = 8
SSD_STATE = 128
SSD_CONV = 4
SSD_CONV_DIM = 1024
GMLP_WIDTH = 256
GMLP_GROUPS = 4
GMLP_CHUNK = 128
ATT_WIDTH = 256
ATT_HEADS = 4
ATT_PREV_CHUNKS = 8
ATT_MAX_REL = 128
N_EXPERTS = 8
TOP_K = 2
RMS_EPS = 1e-6
LN_EPS = 1e-5
NEG = -1e30

LANES = 128
ROW_SUB = 8
VMEM_LIMIT = 56 * 1024 * 1024

_C_Z = 0
_C_XBC = _C_Z + SSD_WIDTH
_C_DT = _C_XBC + SSD_CONV_DIM
_C_U = _C_DT + LANES
_C_V = _C_U + GMLP_WIDTH
_C_Q = _C_V + GMLP_WIDTH
_C_K = _C_Q + ATT_WIDTH
_C_VV = _C_K + ATT_WIDTH
_C_END = _C_VV + ATT_WIDTH

ROW_TILE = 512
SSD_TILE = 512
SSD_L = 256
GMLP_TILE = 512
ATT_TILE = 256
MOE_TM = 1024
MOE_NJ = 2
MOE_SUB = 256
FFN_FC = 256


def _params(sem):
    return pltpu.CompilerParams(dimension_semantics=sem, vmem_limit_bytes=VMEM_LIMIT)


def _rms(x, g):
    y = x * lax.rsqrt(jnp.mean(x * x, axis=-1, keepdims=True) + RMS_EPS)
    return y * g


def _silu(x):
    return x * (1.0 / (1.0 + jnp.exp(-x)))


def _dot(a, b):
    return jnp.dot(a, b, preferred_element_type=F32)


def _dot_nt(a, b):
    return lax.dot_general(a, b, (((1,), (1,)), ((), ())), preferred_element_type=F32)


def _dot_tn(a, b, precision=None):
    return lax.dot_general(a, b, (((0,), (0,)), ((), ())), preferred_element_type=F32,
                           precision=precision)


def _inproj_body(h_ref, g_ref, w_ref, z_ref, xbc_ref, dt_ref, u_ref, v_ref, q_ref, k_ref, vv_ref):
    a = _rms(h_ref[...], g_ref[...]).astype(BF16)

    def proj(lo, hi):
        return _dot(a, w_ref[:, lo:hi])

    z_ref[...] = proj(_C_Z, _C_XBC).astype(BF16)
    xbc_ref[...] = proj(_C_XBC, _C_DT).astype(BF16)
    dt_ref[...] = proj(_C_DT, _C_U)
    u_ref[...] = proj(_C_U, _C_V).astype(BF16)
    v_ref[...] = proj(_C_V, _C_Q).astype(BF16)
    q_ref[...] = proj(_C_Q, _C_K).astype(BF16)
    k_ref[...] = proj(_C_K, _C_VV).astype(BF16)
    vv_ref[...] = proj(_C_VV, _C_END).astype(BF16)


def _in_proj(h, g, w_all, tm=ROW_TILE):
    n, d = h.shape
    widths = [(SSD_WIDTH, BF16), (SSD_CONV_DIM, BF16), (LANES, F32), (GMLP_WIDTH, BF16),
              (GMLP_WIDTH, BF16), (ATT_WIDTH, BF16), (ATT_WIDTH, BF16), (ATT_WIDTH, BF16)]
    row = lambda i: (i, 0)
    full = lambda i: (0, 0)
    return pl.pallas_call(
        _inproj_body,
        grid=(n // tm,),
        in_specs=[pl.BlockSpec((tm, d), row), pl.BlockSpec((1, d), full),
                  pl.BlockSpec(w_all.shape, full)],
        out_specs=[pl.BlockSpec((tm, w), row) for w, _ in widths],
        out_shape=[jax.ShapeDtypeStruct((n, w), dt) for w, dt in widths],
        compiler_params=_params(("arbitrary",)),
        name="in_proj",
    )(h, g, w_all)


def _expand4(q, g, lane):
    c = [q[:, 4 * g + i:4 * g + i + 1] for i in range(4)]
    return jnp.where(lane < 64, c[0], jnp.where(lane < 128, c[1], jnp.where(lane < 192, c[2], c[3])))


def _ssd_body(z_ref, xbc_ref, dt_ref, cw_ref, cbias_ref, dtb_ref, alog_ref, dsk_ref, ng_ref,
              y_ref, h_scr, tail_scr, xe_scr, xc_scr, yacc_scr, *, chunk):
    t = xbc_ref.shape[0]
    gw = SSD_WIDTH // 2

    @pl.when(pl.program_id(1) == 0)
    def _():
        h_scr[...] = jnp.zeros_like(h_scr)
        tail_scr[...] = jnp.zeros_like(tail_scr)

    xcur = xbc_ref[...].astype(F32)
    xe_scr[0:8, :] = tail_scr[...]
    xe_scr[8:8 + t, :] = xcur
    tail_scr[...] = xcur[t - 8:t, :]
    acc = cbias_ref[...] + cw_ref[SSD_CONV - 1:SSD_CONV, :] * xcur
    for k in range(SSD_CONV - 1):
        acc = acc + cw_ref[k:k + 1, :] * xe_scr[pl.ds(8 - (SSD_CONV - 1) + k, t), :]
    xc_scr[...] = _silu(acc)

    dt = dt_ref[...] + dtb_ref[...]
    dt = jnp.maximum(dt, 0.0) + jnp.log1p(jnp.exp(-jnp.abs(dt)))
    da = dt * (-jnp.exp(alog_ref[...]))

    li = lax.broadcasted_iota(jnp.int32, (chunk, chunk), 0)
    si = lax.broadcasted_iota(jnp.int32, (chunk, chunk), 1)
    tril = li >= si
    tri_f = tril.astype(F32)
    lane = lax.broadcasted_iota(jnp.int32, (chunk, gw), 1)
    head_of_lane = lane // HEAD_DIM
    prow = lax.broadcasted_iota(jnp.int32, (gw, LANES), 0) // HEAD_DIM
    pcol = lax.broadcasted_iota(jnp.int32, (gw, LANES), 1)

    for c in range(t // chunk):
        r0 = c * chunk
        da_c = da[r0:r0 + chunk, :]
        dt_c = dt[r0:r0 + chunk, :]
        acs = jnp.dot(tri_f, da_c, preferred_element_type=F32, precision=HIGHEST)
        acs_t = acs.T
        a_tot = acs[chunk - 1:chunk, :]
        e_acs = jnp.exp(acs)
        w_end = dt_c * jnp.exp(a_tot - acs)
        e_tot = jnp.exp(a_tot)
        for g in range(2):
            b0 = SSD_WIDTH + g * SSD_STATE
            c0 = SSD_WIDTH + (2 + g) * SSD_STATE
            xg = xc_scr[r0:r0 + chunk, g * gw:(g + 1) * gw]
            bg = xc_scr[r0:r0 + chunk, b0:b0 + SSD_STATE].astype(BF16)
            cg = xc_scr[r0:r0 + chunk, c0:c0 + SSD_STATE].astype(BF16)
            cb = _dot_nt(cg, bg)
            xdt = xg * _expand4(dt_c, g, lane)
            y = xg * dsk_ref[:, g * gw:(g + 1) * gw]
            for hh in range(4):
                h = 4 * g + hh
                diff = acs[:, h:h + 1] - acs_t[h:h + 1, :]
                m_h = (cb * jnp.where(tril, jnp.exp(diff), 0.0)).astype(BF16)
                x_h = jnp.where(head_of_lane == hh, xdt, 0.0).astype(BF16)
                y = y + _dot(m_h, x_h)
            h_old = h_scr[g]
            y = y + _dot_nt(cg, h_old.astype(BF16)) * _expand4(e_acs, g, lane)
            xw = (xg * _expand4(w_end, g, lane)).astype(BF16)
            s_new = _dot_tn(xw, bg)
            rowscale = jnp.sum(jnp.where(pcol == prow + 4 * g, e_tot, 0.0), axis=1, keepdims=True)
            h_scr[g] = h_old * rowscale + s_new
            zg = z_ref[r0:r0 + chunk, g * gw:(g + 1) * gw].astype(F32)
            yacc_scr[r0:r0 + chunk, g * gw:(g + 1) * gw] = y * _silu(zg)

    y_ref[...] = _rms(yacc_scr[...], ng_ref[...]).astype(y_ref.dtype)


def _ssd(z, xbc, dt, conv_w, conv_b, dt_bias, a_log, d_lane, norm_g, batch, seq,
         tile=SSD_TILE, chunk=SSD_L):
    n = batch * seq
    nt = seq // tile
    row = lambda b, i: (b * nt + i, 0)
    full = lambda b, i: (0, 0)
    return pl.pallas_call(
        functools.partial(_ssd_body, chunk=chunk),
        grid=(batch, nt),
        in_specs=[pl.BlockSpec((tile, SSD_WIDTH), row), pl.BlockSpec((tile, SSD_CONV_DIM), row),
                  pl.BlockSpec((tile, LANES), row),
                  pl.BlockSpec((SSD_CONV, SSD_CONV_DIM), full), pl.BlockSpec((1, SSD_CONV_DIM), full),
                  pl.BlockSpec((1, LANES), full), pl.BlockSpec((1, LANES), full),
                  pl.BlockSpec((1, SSD_WIDTH), full), pl.BlockSpec((1, SSD_WIDTH), full)],
        out_specs=pl.BlockSpec((tile, SSD_WIDTH), row),
        out_shape=jax.ShapeDtypeStruct((n, SSD_WIDTH), BF16),
        scratch_shapes=[pltpu.VMEM((2, SSD_WIDTH // 2, SSD_STATE), F32),
                        pltpu.VMEM((8, SSD_CONV_DIM), F32),
                        pltpu.VMEM((tile + 8, SSD_CONV_DIM), F32),
                        pltpu.VMEM((tile, SSD_CONV_DIM), F32),
                        pltpu.VMEM((tile, SSD_WIDTH), F32)],
        compiler_params=_params(("arbitrary", "arbitrary")),
        name="ssd",
    )(z, xbc, dt, conv_w, conv_b, dt_bias, a_log, d_lane, norm_g)


def _gmlp_body(u_ref, v_ref, lng_ref, lnb_ref, ws_ref, bias_ref, og_ref, y_ref):
    t = u_ref.shape[0]
    blk = GMLP_CHUNK
    ti = lax.broadcasted_iota(jnp.int32, (blk, blk), 0)
    si = lax.broadcasted_iota(jnp.int32, (blk, blk), 1)
    w = [jnp.where(ti >= si, ws_ref[g], 0.0).astype(BF16) for g in range(GMLP_GROUPS)]
    grp = lax.broadcasted_iota(jnp.int32, (blk, GMLP_WIDTH), 1) // HEAD_DIM
    for c in range(t // blk):
        r0 = c * blk
        v = v_ref[r0:r0 + blk, :].astype(F32)
        mu = jnp.mean(v, axis=-1, keepdims=True)
        vc = v - mu
        var = jnp.mean(vc * vc, axis=-1, keepdims=True)
        vn = (vc * lax.rsqrt(var + LN_EPS) * lng_ref[...] + lnb_ref[...]).astype(BF16)
        mixed = bias_ref[...]
        for g in range(GMLP_GROUPS):
            mixed = mixed + jnp.where(grp == g, _dot(w[g], vn), 0.0)
        out = u_ref[r0:r0 + blk, :].astype(F32) * mixed
        y_ref[r0:r0 + blk, :] = _rms(out, og_ref[...]).astype(y_ref.dtype)


def _gmlp(u, v, ln_g, ln_b, w_s, bias_tc, out_g, tile=GMLP_TILE):
    n = u.shape[0]
    row = lambda i: (i, 0)
    full2 = lambda i: (0, 0)
    return pl.pallas_call(
        _gmlp_body,
        grid=(n // tile,),
        in_specs=[pl.BlockSpec((tile, GMLP_WIDTH), row), pl.BlockSpec((tile, GMLP_WIDTH), row),
                  pl.BlockSpec((1, GMLP_WIDTH), full2), pl.BlockSpec((1, GMLP_WIDTH), full2),
                  pl.BlockSpec(w_s.shape, lambda i: (0, 0, 0)),
                  pl.BlockSpec((GMLP_CHUNK, GMLP_WIDTH), full2), pl.BlockSpec((1, GMLP_WIDTH), full2)],
        out_specs=pl.BlockSpec((tile, GMLP_WIDTH), row),
        out_shape=jax.ShapeDtypeStruct((n, GMLP_WIDTH), BF16),
        compiler_params=_params(("arbitrary",)),
        name="gmlp",
    )(u, v, ln_g, ln_b, w_s, bias_tc, out_g)


def _attn_body(q_ref, k0_ref, k1_ref, k2_ref, v0_ref, v1_ref, v2_ref, bias_ref, og_ref, y_ref):
    tq = q_ref.shape[0]
    nk = 3 * tq
    start = pl.program_id(1) * tq
    q = q_ref[...]
    kcat = jnp.concatenate([k0_ref[...], k1_ref[...], k2_ref[...]], axis=0)
    vcat = jnp.concatenate([v0_ref[...], v1_ref[...], v2_ref[...]], axis=0)
    kk = lax.broadcasted_iota(jnp.int32, (tq, nk), 1)
    key_ok = kk >= 2 * tq - start
    head_of_lane = lax.broadcasted_iota(jnp.int32, (tq, ATT_WIDTH), 1) // HEAD_DIM
    out = jnp.zeros((tq, ATT_WIDTH), F32)
    for h in range(ATT_HEADS):
        qh = jnp.where(head_of_lane == h, q, jnp.zeros_like(q))
        s = _dot_nt(qh, kcat) + bias_ref[h]
        s = jnp.where(key_ok, s, NEG)
        m = jnp.max(s, axis=-1, keepdims=True)
        p = jnp.exp(s - m)
        l = jnp.sum(p, axis=-1, keepdims=True)
        o = _dot(p.astype(BF16), vcat) / l
        out = jnp.where(head_of_lane == h, o, out)
    y_ref[...] = _rms(out, og_ref[...]).astype(y_ref.dtype)


def _attn(q, k, v, bias_tab, out_g, batch, seq, tile=ATT_TILE):
    n = batch * seq
    nt = seq // tile
    cur = lambda b, i: (b * nt + i, 0)
    prev1 = lambda b, i: (b * nt + jnp.maximum(i - 1, 0), 0)
    prev2 = lambda b, i: (b * nt + jnp.maximum(i - 2, 0), 0)
    blk = lambda m: pl.BlockSpec((tile, ATT_WIDTH), m)
    return pl.pallas_call(
        _attn_body,
        grid=(batch, nt),
        in_specs=[blk(cur), blk(prev2), blk(prev1), blk(cur), blk(prev2), blk(prev1), blk(cur),
                  pl.BlockSpec(bias_tab.shape, lambda b, i: (0, 0, 0)),
                  pl.BlockSpec((1, ATT_WIDTH), lambda b, i: (0, 0))],
        out_specs=blk(cur),
        out_shape=jax.ShapeDtypeStruct((n, ATT_WIDTH), BF16),
        compiler_params=_params(("arbitrary", "arbitrary")),
        name="band_attn",
    )(q, k, k, k, v, v, v, bias_tab, out_g)


def _attn_bias_table(rel_bias, tile=ATT_TILE):
    tq, nk = tile, 3 * tile
    period = tq + nk
    mp = jnp.arange(period)
    m = jnp.where(mp < nk, mp, mp - period)
    idx = jnp.clip(2 * tile - m, -ATT_MAX_REL, ATT_MAX_REL) + ATT_MAX_REL
    per = rel_bias[:, idx].astype(F32)
    heads = rel_bias.shape[0]
    toep = jnp.tile(per, (1, tq))[:, :tq * (period - 1)].reshape(heads, tq, period - 1)[:, :, :nk]
    qq = jnp.arange(tq)[:, None]
    kk = jnp.arange(nk)[None, :]
    qc = qq // CHUNK + (2 * tile) // CHUNK
    kc = kk // CHUNK
    band = (kc <= qc) & (kc >= qc - ATT_PREV_CHUNKS)
    return jnp.where(band[None], toep, NEG)


def _outproj_body(*refs, with_router):
    if with_router:
        (ys_ref, yg_ref, ya_ref, h_ref, w_ref, gpost_ref, gpre_ref, rwh_ref, rwl_ref,
         hn_ref, f_ref, idx_ref, gate_ref) = refs
    else:
        ys_ref, yg_ref, ya_ref, h_ref, w_ref, gpost_ref, gpre_ref, hn_ref, f_ref = refs
    s1 = SSD_WIDTH
    s2 = s1 + GMLP_WIDTH
    mixed = (_dot(ys_ref[...], w_ref[0:s1, :]) + _dot(yg_ref[...], w_ref[s1:s2, :])
             + _dot(ya_ref[...], w_ref[s2:, :]))
    hn = h_ref[...] + _rms(mixed, gpost_ref[...])
    hn_ref[...] = hn
    f = _rms(hn, gpre_ref[...])
    if with_router:
        tm, d = f.shape
        for c in range(d // LANES):
            f_ref[pl.ds(c, tm, stride=ROW_SUB), :] = f[:, c * LANES:(c + 1) * LANES]
    else:
        f_ref[...] = f.astype(f_ref.dtype)
    if with_router:
        f_hi = f.astype(BF16)
        f_lo = (f - f_hi.astype(F32)).astype(BF16)
        logits = (_dot(f_hi, rwh_ref[...]) + _dot(f_lo, rwh_ref[...])
                  + _dot(f_hi, rwl_ref[...]))
        lane =lax.broadcasted_iota(jnp.int32, logits.shape, 1)
        lane_f = lane.astype(F32)
        logits = jnp.where(lane < N_EXPERTS, logits, -jnp.inf)
        t1 = jnp.max(logits, axis=-1, keepdims=True)
        i1 = jnp.min(jnp.where(logits == t1, lane_f, float(LANES)), axis=-1, keepdims=True)
        rest = jnp.where(lane_f == i1, -jnp.inf, logits)
        t2 = jnp.max(rest, axis=-1, keepdims=True)
        i2 = jnp.min(jnp.where(rest == t2, lane_f, float(LANES)), axis=-1, keepdims=True)
        e2 = jnp.exp(t2 - t1)
        g1 = 1.0 / (1.0 + e2)
        g2 = e2 / (1.0 + e2)
        idx_ref[...] = jnp.where(lane == 0, i1, i2).astype(jnp.int32)
        gate_ref[...] = jnp.where(lane == 0, g1, g2)


def _out_proj(ys, yg, ya, h, w_out, g_post, g_pre, router_w=None, tm=ROW_TILE):
    n, d = h.shape
    with_router = router_w is not None
    row = lambda i: (i, 0)
    full = lambda i: (0, 0)
    in_specs = [pl.BlockSpec((tm, SSD_WIDTH), row), pl.BlockSpec((tm, GMLP_WIDTH), row),
                pl.BlockSpec((tm, ATT_WIDTH), row), pl.BlockSpec((tm, d), row),
                pl.BlockSpec(w_out.shape, full), pl.BlockSpec((1, d), full), pl.BlockSpec((1, d), full)]
    if with_router:
        f_spec = pl.BlockSpec((tm * ROW_SUB, LANES), row)
        f_shape = jax.ShapeDtypeStruct((n * ROW_SUB, LANES), F32)
    else:
        f_spec = pl.BlockSpec((tm, d), row)
        f_shape = jax.ShapeDtypeStruct((n, d), BF16)
    out_specs = [pl.BlockSpec((tm, d), row), f_spec]
    out_shape = [jax.ShapeDtypeStruct((n, d), F32), f_shape]
    args = [ys, yg, ya, h, w_out, g_post, g_pre]
    if with_router:
        rw_hi = router_w.astype(BF16)
        rw_lo = (router_w - rw_hi.astype(F32)).astype(BF16)
        in_specs += [pl.BlockSpec(router_w.shape, full), pl.BlockSpec(router_w.shape, full)]
        out_specs += [pl.BlockSpec((tm, LANES), row), pl.BlockSpec((tm, LANES), row)]
        out_shape += [jax.ShapeDtypeStruct((n, LANES), jnp.int32), jax.ShapeDtypeStruct((n, LANES), F32)]
        args += [rw_hi, rw_lo]
    return pl.pallas_call(
        functools.partial(_outproj_body, with_router=with_router),
        grid=(n // tm,),
        in_specs=in_specs, out_specs=out_specs, out_shape=out_shape,
        compiler_params=_params(("arbitrary",)),
        name="out_proj_router" if with_router else "out_proj",
    )(*args)


def _ffn_body(f_ref, h_ref, wg_ref, wu_ref, wd_ref, gpost_ref, o_ref, acc_ref, *, fc):
    x = f_ref[...]
    d_ff = wg_ref.shape[1]
    for c in range(d_ff // fc):
        lo = c * fc
        mid = (_silu(_dot(x, wg_ref[:, lo:lo + fc])) * _dot(x, wu_ref[:, lo:lo + fc])).astype(BF16)
        part = _dot(mid, wd_ref[lo:lo + fc, :])
        if c == 0:
            acc_ref[...] = part
        else:
            acc_ref[...] += part
    o_ref[...] = h_ref[...] + _rms(acc_ref[...], gpost_ref[...])


def _ffn_dense(f, h, wg, wu, wd, g_post, tm=ROW_TILE, fc=FFN_FC):
    n, d = h.shape
    row = lambda i: (i, 0)
    full = lambda i: (0, 0)
    return pl.pallas_call(
        functools.partial(_ffn_body, fc=fc),
        grid=(n // tm,),
        in_specs=[pl.BlockSpec((tm, d), row), pl.BlockSpec((tm, d), row),
                  pl.BlockSpec(wg.shape, full), pl.BlockSpec(wu.shape, full),
                  pl.BlockSpec(wd.shape, full), pl.BlockSpec((1, d), full)],
        out_specs=pl.BlockSpec((tm, d), row),
        out_shape=jax.ShapeDtypeStruct((n, d), F32),
        scratch_shapes=[pltpu.VMEM((tm, d), F32)],
        compiler_params=_params(("arbitrary",)),
        name="ffn_dense",
    )(f, h, wg, wu, wd, g_post)


def _moe_body(te_ref, nu_ref, tok_ref, tokn_ref, dstp_ref, x_hbm, wgu_ref, wd_ref, out_hbm,
              xbuf, xb16, mid_scr, ybuf, acc_ref, sems, *, nj, sub):
    i = pl.program_id(0)
    j = pl.program_id(1)
    tm, d = xb16.shape
    rps = tm // nj
    n_used = nu_ref[0]
    used = i < n_used
    n_chunks = mid_scr.shape[1] // sub
    gsem = sems.at[0]
    ssem = sems.at[1]

    def row_of(buf, r):
        return buf.at[pl.ds(pl.multiple_of(r * ROW_SUB, ROW_SUB), ROW_SUB)]

    def tile_of(hbm, idx):
        return hbm.at[pl.ds(pl.multiple_of(idx, ROW_SUB), ROW_SUB)]

    def gather_copy(idx, r):
        return pltpu.make_async_copy(tile_of(x_hbm, idx), row_of(xbuf, r), gsem)

    def scatter_copy(idx, r):
        return pltpu.make_async_copy(row_of(ybuf, r), tile_of(out_hbm, idx), ssem)

    def gather_start(idx_ref, r):
        gather_copy(idx_ref[0, 0, r], r).start()

    def scatter_start(r):
        scatter_copy(dstp_ref[0, 0, r], r).start()

    @pl.when(jnp.logical_and(j == 0, i <= n_used))
    def _():
        @pl.when(i == 0)
        def _():
            def body(r, carry):
                gather_start(tok_ref, r)
                return carry
            lax.fori_loop(0, tm, body, 0, unroll=8)
            ybuf[...] = jnp.zeros_like(ybuf)
            n_out = out_hbm.shape[0]
            fills = [pltpu.make_async_copy(ybuf, out_hbm.at[pl.ds(n_out - (s + 1) * tm * ROW_SUB, tm * ROW_SUB)], ssem)
                     for s in range(N_EXPERTS)]
            for cp in fills:
                cp.start()
            for cp in fills:
                cp.wait()

        def wbody(r, carry):
            gather_copy(0, r).wait()
            return carry
        lax.fori_loop(0, tm, wbody, 0, unroll=8)
        for c in range(d // LANES):
            xb16[:, c * LANES:(c + 1) * LANES] = xbuf[pl.ds(c, tm, stride=ROW_SUB), :].astype(BF16)

    @pl.when(used)
    def _():
        x = xb16[...]
        base = j * rps
        for k in range(rps):
            gather_start(tokn_ref, base + k)
            scatter_start(base + k)
        for c in range(n_chunks):
            gu = _dot(x, wgu_ref[0, 0, :, c * 2 * sub:(c + 1) * 2 * sub])
            mid_scr[:, c * sub:(c + 1) * sub] = (_silu(gu[:, :sub]) * gu[:, sub:]).astype(BF16)
        part = _dot(mid_scr[...], wd_ref[0])

        @pl.when(j == 0)
        def _():
            acc_ref[...] = part

        @pl.when(j > 0)
        def _():
            acc_ref[...] += part

    @pl.when(jnp.logical_and(jnp.logical_not(used), i == n_used))
    def _():
        def body(r, carry):
            scatter_start(j * rps + r)
            return carry
        lax.fori_loop(0, rps, body, 0, unroll=8)

    @pl.when(jnp.logical_and(j == nj - 1, i <= n_used))
    def _():
        def wbody(r, carry):
            scatter_copy(0, r).wait()
            return carry
        lax.fori_loop(0, tm, wbody, 0, unroll=8)

        @pl.when(used)
        def _():
            for c in range(d // LANES):
                ybuf[pl.ds(c, tm, stride=ROW_SUB), :] = acc_ref[:, c * LANES:(c + 1) * LANES]


def _moe_experts(tile_expert, n_used, row_tok, row_dst, x, wgu, wd, n_out_rows, tm):
    n_tiles = row_tok.shape[0]
    d = wd.shape[2]
    nj = wgu.shape[1]
    tf = wd.shape[1] // nj
    last = n_tiles - 1
    smem = lambda m: pl.BlockSpec((1, 1, tm), m, memory_space=pltpu.SMEM)
    jeff = lambda i, j, nu: jnp.where(i < nu[0], j, nj - 1)
    grid_spec = pltpu.PrefetchScalarGridSpec(
        num_scalar_prefetch=2,
        grid=(n_tiles, nj),
        in_specs=[smem(lambda i, j, te, nu: (i, 0, 0)),
                  smem(lambda i, j, te, nu: (jnp.minimum(i + 1, last), 0, 0)),
                  smem(lambda i, j, te, nu: (jnp.maximum(i - 1, 0), 0, 0)),
                  pl.BlockSpec(memory_space=pl.ANY),
                  pl.BlockSpec((1, 1, d, 2 * tf), lambda i, j, te, nu: (te[i], jeff(i, j, nu), 0, 0)),
                  pl.BlockSpec((1, tf, d), lambda i, j, te, nu: (te[i], jeff(i, j, nu), 0))],
        out_specs=pl.BlockSpec(memory_space=pl.ANY),
        scratch_shapes=[pltpu.VMEM((tm * ROW_SUB, LANES), F32), pltpu.VMEM((tm, d), BF16),
                        pltpu.VMEM((tm, tf), BF16), pltpu.VMEM((tm * ROW_SUB, LANES), F32),
                        pltpu.VMEM((tm, d), F32),
                        pltpu.SemaphoreType.DMA((2,))],
    )
    return pl.pallas_call(
        functools.partial(_moe_body, nj=nj, sub=MOE_SUB),
        grid_spec=grid_spec,
        out_shape=jax.ShapeDtypeStruct((n_out_rows * ROW_SUB, LANES), F32),
        compiler_params=_params(("arbitrary", "arbitrary")),
        name="moe_experts",
    )(tile_expert, n_used, row_tok, row_tok, row_dst, x, wgu, wd)


def _combine_body(y0_ref, y1_ref, gate_ref, h_ref, gpost_ref, o_ref):
    tm, d = o_ref.shape
    gate = gate_ref[...]
    g0 = gate[:, 0:1]
    g1 = gate[:, 1:2]
    parts = []
    ss = jnp.zeros((tm, 1), F32)
    for c in range(d // LANES):
        rows = pl.ds(c, tm, stride=ROW_SUB)
        f = y0_ref[rows, :] * g0 + y1_ref[rows, :] * g1
        ss = ss + jnp.sum(f * f, axis=-1, keepdims=True)
        parts.append(f)
    scale = lax.rsqrt(ss / d + RMS_EPS)
    for c, f in enumerate(parts):
        cols = slice(c * LANES, (c + 1) * LANES)
        o_ref[:, cols] = h_ref[:, cols] + f * scale * gpost_ref[:, cols]


def _moe_combine(y_slots, gates, h, g_post, tm=ROW_TILE):
    n, d = h.shape
    nt = n // tm
    row = lambda i: (i, 0)
    return pl.pallas_call(
        _combine_body,
        grid=(nt,),
        in_specs=[pl.BlockSpec((tm * ROW_SUB, LANES), row),
                  pl.BlockSpec((tm * ROW_SUB, LANES), lambda i: (i + nt, 0)),
                  pl.BlockSpec((tm, LANES), row), pl.BlockSpec((tm, d), row),
                  pl.BlockSpec((1, d), lambda i: (0, 0))],
        out_specs=pl.BlockSpec((tm, d), row),
        out_shape=jax.ShapeDtypeStruct((n, d), F32),
        compiler_params=_params(("arbitrary",)),
        name="moe_combine",
    )(y_slots, y_slots, gates, h, g_post)


def _moe_weights(w_gate, w_up, w_down, nj=MOE_NJ, sub=MOE_SUB):
    e, d, d_ff = w_gate.shape
    tf = d_ff // nj
    shape = (e, d, nj, tf // sub, 1, sub)
    wgu = jnp.concatenate([w_gate.astype(BF16).reshape(shape), w_up.astype(BF16).reshape(shape)], axis=4)
    wgu = wgu.transpose(0, 2, 1, 3, 4, 5).reshape(e, nj, d, 2 * tf)
    return wgu, w_down.astype(BF16)


def _moe_layer(f, h, top_idx, gates, wgu, wd, g_post, tm=MOE_TM):
    n = h.shape[0]
    n_assign = n * TOP_K
    n_tiles = n_assign // tm + N_EXPERTS
    experts = jnp.arange(N_EXPERTS, dtype=jnp.int32)
    flat_e = top_idx.reshape(-1)
    onehot = (flat_e[:, None] == experts[None, :]).astype(jnp.int32)
    csum = jnp.cumsum(onehot, axis=0)
    counts = csum[-1]
    padded = (counts + tm - 1) // tm * tm
    pad_end = jnp.cumsum(padded)
    pad_start = pad_end - padded
    dest = jnp.sum((csum - onehot + pad_start[None, :]) * onehot, axis=1).astype(jnp.int32)
    row_asg = jnp.full((n_tiles * tm,), -1, jnp.int32).at[dest].set(
        jnp.arange(n_assign, dtype=jnp.int32)).reshape(n_tiles, tm)
    tile_first = jnp.arange(n_tiles, dtype=jnp.int32) * tm
    tile_e = jnp.minimum(jnp.sum(pad_end[None, :] <= tile_first[:, None], axis=1),
                         N_EXPERTS - 1).astype(jnp.int32)
    n_used = (pad_end[-1] // tm).astype(jnp.int32)
    spare_before = jnp.cumsum(padded - counts) - (padded - counts)
    tile_oh = (tile_e[:, None] == experts[None, :]).astype(jnp.int32)
    spare_base = jnp.sum(tile_oh * (pad_start + counts - spare_before)[None, :], axis=1)
    row = tile_first[:, None] + jnp.arange(tm, dtype=jnp.int32)[None, :]
    spare = jnp.clip(row - spare_base[:, None], 0, N_EXPERTS * tm - 1)
    valid = row_asg >= 0
    asg = jnp.maximum(row_asg, 0)
    row_tok = (asg // TOP_K * ROW_SUB).reshape(n_tiles, 1, tm)
    row_dst = (jnp.where(valid, (asg % TOP_K) * n + asg // TOP_K, TOP_K * n + spare)
               * ROW_SUB).reshape(n_tiles, 1, tm)
    tile_expert = jnp.where(jnp.arange(n_tiles) < n_used, tile_e, tile_e[n_used - 1])
    y_slots = _moe_experts(tile_expert, n_used.reshape(1), row_tok, row_dst, f, wgu, wd,
                           TOP_K * n + N_EXPERTS * tm, tm)
    return _moe_combine(y_slots, gates, h, g_post)


def _pad_lanes(x, width=LANES):
    return jnp.pad(x, ((0, 0), (0, width - x.shape[-1])))


def kernel(x, mix_pre_norm, mix_post_norm, ffn_pre_norm, ffn_post_norm, w_in, ssd_conv_w, ssd_conv_b, ssd_dt_bias, ssd_a_log, ssd_d, ssd_norm, gmlp_ln_g, gmlp_ln_b, gmlp_w_s, gmlp_b_s, gmlp_out_norm, attn_rel_bias, attn_out_norm, w_out, ffn_w_gate, ffn_w_up, ffn_w_down, moe_router, moe_w_gate, moe_w_up, moe_w_down):
    batch, seq, d = x.shape
    depth = w_in.shape[0]
    n = batch * seq
    h = x.reshape(n, d)
    s1 = SSD_WIDTH
    s2 = s1 + SSD_CONV_DIM
    s3 = s2 + SSD_HEADS
    s4 = s3 + GMLP_WIDTH
    s5 = s4 + GMLP_WIDTH
    for layer in range(depth):
        wl = w_in[layer]
        w_all = jnp.concatenate([
            wl[:, :s2], _pad_lanes(wl[:, s2:s3]), wl[:, s3:s5],
            wl[:, s5:s5 + ATT_WIDTH] * (HEAD_DIM ** -0.5), wl[:, s5 + ATT_WIDTH:]], axis=1).astype(BF16)
        z, xbc, dt, u, v, q, k, vv = _in_proj(h, mix_pre_norm[layer][None], w_all)
        y_ssd = _ssd(z, xbc, dt, ssd_conv_w[layer], ssd_conv_b[layer][None],
                     _pad_lanes(ssd_dt_bias[layer][None]), _pad_lanes(ssd_a_log[layer][None]),
                     jnp.repeat(ssd_d[layer], HEAD_DIM)[None], ssd_norm[layer][None], batch, seq)
        y_gmlp = _gmlp(u, v, gmlp_ln_g[layer][None], gmlp_ln_b[layer][None], gmlp_w_s[layer],
                       jnp.repeat(gmlp_b_s[layer].T, HEAD_DIM, axis=1), gmlp_out_norm[layer][None])
        y_att = _attn(q, k, vv, _attn_bias_table(attn_rel_bias[layer]), attn_out_norm[layer][None],
                      batch, seq)
        wo = w_out[layer].astype(BF16)
        i = layer // 2
        if layer % 2 == 0:
            h, f = _out_proj(y_ssd, y_gmlp, y_att, h, wo, mix_post_norm[layer][None],
                             ffn_pre_norm[layer][None])
            h = _ffn_dense(f, h, ffn_w_gate[i].astype(BF16), ffn_w_up[i].astype(BF16),
                           ffn_w_down[i].astype(BF16), ffn_post_norm[layer][None])
        else:
            h, f, idx, gate = _out_proj(y_ssd, y_gmlp, y_att, h, wo, mix_post_norm[layer][None],
                                        ffn_pre_norm[layer][None], _pad_lanes(moe_router[i]))
            wgu, wd = _moe_weights(moe_w_gate[i], moe_w_up[i], moe_w_down[i])
            h = _moe_layer(f, h, idx[:, :TOP_K], gate, wgu, wd, ffn_post_norm[layer][None])
    return h.reshape(batch, seq, d)
```
